```python
import functools
import jax, jax.numpy as jnp
from jax import lax
import numpy as np

D_MODEL = 2048
BATCH = 4
SEQ = 2048
DEPTH = 1
DEC_BATCH = 8
DEC_SEQ = 4
PAST_LEN = 16384
PAGE_SIZE = 128

MIX_WIDTH = D_MODEL
GLA_WIDTH = MIX_WIDTH // 2
GLA_HEADS = 4
GLA_DV = GLA_WIDTH // GLA_HEADS
GLA_DK = GLA_DV // 2
GLA_GATE_RANK = 16
GLA_TAU = 16.0
GLA_CHUNK = 16
MOBA_WIDTH = MIX_WIDTH - GLA_WIDTH
MOBA_HEADS = 8
MOBA_DH = MOBA_WIDTH // MOBA_HEADS
MOBA_BLOCK = 256
MOBA_TOPK = 3
MOBA_QBLOCK = 16
PEER_HEADS = 8
PEER_NKEYS = 128
PEER_EXPERTS = PEER_NKEYS * PEER_NKEYS
PEER_TOPK = 16
PEER_QDIM = 128
PEER_TOKEN_BLOCK = 128
PLE_DIM = 256
RMS_EPS = 1e-6
IN_WIDTHS = (GLA_HEADS * GLA_DK, GLA_HEADS * GLA_DK, GLA_WIDTH, GLA_WIDTH, GLA_GATE_RANK,
             MOBA_WIDTH, MOBA_WIDTH, MOBA_WIDTH)
IN_TOTAL = sum(IN_WIDTHS)

kernel_name = 'hymba_gla_moba_peer_decode_step'


def rmsnorm(x, g):
    xf = x.astype(jnp.float32)
    y = xf * lax.rsqrt(jnp.mean(xf * xf, axis=-1, keepdims=True) + RMS_EPS)
    return (y * g.astype(jnp.float32)).astype(x.dtype)


def alibi_slopes():
    return 2.0 ** (-8.0 * jnp.arange(1, MOBA_HEADS + 1, dtype=jnp.float32) / MOBA_HEADS)


def gla_recurrence(q, k, v, log_a, s0):
    B, T, H, _ = q.shape
    pad = (-T) % GLA_CHUNK
    n = (T + pad) // GLA_CHUNK

    def to_blocks(t):
        t = jnp.pad(t.astype(jnp.float32), ((0, 0), (0, pad), (0, 0), (0, 0)))
        return t.reshape(B, n, GLA_CHUNK, H, t.shape[-1]).transpose(1, 0, 3, 2, 4)

    causal = jnp.tril(jnp.ones((GLA_CHUNK, GLA_CHUNK), dtype=bool))[:, :, None]

    def step(S, blk):
        qc, kc, vc, ac = blk
        b = jnp.cumsum(ac, axis=2)
        decay = jnp.exp(jnp.where(causal, b[:, :, :, None, :] - b[:, :, None, :, :], -jnp.inf))
        scores = jnp.einsum('bhtsd,bhsd->bhts', qc[:, :, :, None, :] * decay, kc)
        o = jnp.einsum('bhtd,bhde->bhte', qc * jnp.exp(b), S) + jnp.einsum('bhts,bhse->bhte', scores, vc)
        b_end = b[:, :, -1:, :]
        S = jnp.exp(b_end[:, :, 0, :])[..., None] * S + jnp.einsum('bhsd,bhse->bhde', kc * jnp.exp(b_end - b), vc)
        return S, o

    s_end, o = lax.scan(step, s0.astype(jnp.float32),
                        (to_blocks(q), to_blocks(k), to_blocks(v), to_blocks(log_a)))
    o = o.transpose(1, 0, 3, 2, 4).reshape(B, n * GLA_CHUNK, H, -1)[:, :T]
    return o, s_end


def moba_select(q, means, n_eligible):
    nbf = means.shape[2]
    s = jnp.einsum('bhqd,bhcd->bhqc', q.astype(jnp.float32), means)
    s = jnp.where(jnp.arange(nbf) < n_eligible, s, -jnp.inf)
    vals, idx = lax.top_k(s, min(MOBA_TOPK, nbf))
    return idx, jnp.isfinite(vals)


def gather_blocks(blocks, idx):
    return jax.vmap(jax.vmap(lambda t, i: t[i]))(blocks, idx)


def moba_attend(q, q_pos, own_k, own_v, own_pos, sel_k=None, sel_v=None, sel_pos=None, sel_valid=None):
    slopes = alibi_slopes()[None, :, None, None]
    scale = MOBA_DH ** -0.5
    dist_own = (q_pos[:, None] - own_pos[None, :]).astype(jnp.float32)
    s_own = jnp.einsum('bhqd,bhkd->bhqk', q, own_k).astype(jnp.float32) * scale - slopes * dist_own
    s_own = jnp.where(dist_own >= 0, s_own, -jnp.inf)
    if sel_k is None:
        p = jax.nn.softmax(s_own, axis=-1)
        return jnp.einsum('bhqk,bhkd->bhqd', p, own_v)
    B, H, Q, S, BL, _ = sel_k.shape
    dist_sel = (q_pos[:, None, None] - sel_pos).astype(jnp.float32)
    s_sel = jnp.einsum('bhqd,bhqnkd->bhqnk', q, sel_k).astype(jnp.float32) * scale - slopes[..., None] * dist_sel
    s_sel = jnp.where(sel_valid[..., None], s_sel, -jnp.inf).reshape(B, H, Q, S * BL)
    p = jax.nn.softmax(jnp.concatenate([s_sel, s_own], axis=-1), axis=-1)
    p_sel = p[..., :S * BL].reshape(B, H, Q, S, BL)
    p_own = p[..., S * BL:]
    return jnp.einsum('bhqnk,bhqnkd->bhqd', p_sel, sel_v) + jnp.einsum('bhqk,bhkd->bhqd', p_own, own_v)


def moba_prompt(q, k, v):
    B, T, H, dh = q.shape
    nb = -(-T // MOBA_BLOCK)
    nbf = T // MOBA_BLOCK
    pad = nb * MOBA_BLOCK - T

    def to_blocks(t):
        t = jnp.pad(t, ((0, 0), (0, pad), (0, 0), (0, 0)))
        return t.reshape(B, nb, MOBA_BLOCK, H, dh).transpose(0, 3, 1, 2, 4)

    kb, vb = to_blocks(k), to_blocks(v)
    means = kb[:, :, :nbf].astype(jnp.float32).mean(axis=3)
    nq = T // MOBA_QBLOCK
    qb = q.reshape(B, nq, MOBA_QBLOCK, H, dh).transpose(1, 0, 3, 2, 4)

    def one(args):
        qi, qc = args
        start = qi * MOBA_QBLOCK
        q_pos = start + jnp.arange(MOBA_QBLOCK)
        j = start // MOBA_BLOCK
        own_k = lax.dynamic_index_in_dim(kb, j, axis=2, keepdims=False)
        own_v = lax.dynamic_index_in_dim(vb, j, axis=2, keepdims=False)
        own_pos = j * MOBA_BLOCK + jnp.arange(MOBA_BLOCK)
        if nbf == 0:
            return moba_attend(qc, q_pos, own_k, own_v, own_pos)
        idx, valid = moba_select(qc, means, j)
        sel_pos = idx[..., None] * MOBA_BLOCK + jnp.arange(MOBA_BLOCK)
        return moba_attend(qc, q_pos, own_k, own_v, own_pos,
                           gather_blocks(kb, idx), gather_blocks(vb, idx), sel_pos, valid)

    o = lax.map(one, (jnp.arange(nq), qb))
    return o.transpose(1, 0, 3, 2, 4).reshape(B, T, H * dh)


def moba_sample(q, k, v, cache_k, cache_v, page_table):
    DB, Tq, H, dh = q.shape
    P = page_table.shape[1] * PAGE_SIZE
    j = P // MOBA_BLOCK
    nbf = j
    q_pos = P + jnp.arange(Tq)
    qh = q.transpose(0, 2, 1, 3)
    past_rows = jnp.arange(j * MOBA_BLOCK, P)

    def paged_rows(cache):
        return cache[page_table[:, past_rows // PAGE_SIZE], past_rows % PAGE_SIZE]

    own_k = jnp.concatenate([paged_rows(cache_k), k], axis=1).transpose(0, 2, 1, 3)
    own_v = jnp.concatenate([paged_rows(cache_v), v], axis=1).transpose(0, 2, 1, 3)
    own_pos = jnp.concatenate([past_rows, q_pos])
    if nbf == 0:
        o = moba_attend(qh, q_pos, own_k, own_v, own_pos)
    else:
        ppb = MOBA_BLOCK // PAGE_SIZE
        page_sums = cache_k.astype(jnp.float32).sum(axis=1)
        means = page_sums[page_table[:, :nbf * ppb]].reshape(DB, nbf, ppb, H, dh).sum(axis=2) / MOBA_BLOCK
        means = means.transpose(0, 2, 1, 3)
        idx, valid = moba_select(qh, means, nbf)
        sel_pos = idx[..., None] * MOBA_BLOCK + jnp.arange(MOBA_BLOCK)
        b_ix = jnp.arange(DB)[:, None, None, None, None]
        h_ix = jnp.arange(H)[None, :, None, None, None]
        pages = page_table[b_ix, sel_pos // PAGE_SIZE]
        offs = sel_pos % PAGE_SIZE
        sel_k = cache_k[pages, offs, h_ix]
        sel_v = cache_v[pages, offs, h_ix]
        o = moba_attend(qh, q_pos, own_k, own_v, own_pos, sel_k, sel_v, sel_pos, valid)
    return o.transpose(0, 2, 1, 3).reshape(DB, Tq, H * dh)


def peer_tokens(c, w_q, sub_keys, u_tab, v_tab):
    n = c.shape[0]
    q = (c @ w_q).reshape(n, PEER_HEADS, 2, PEER_QDIM // 2).astype(jnp.float32)
    s = jnp.einsum('nhpd,hpkd->nhpk', q, sub_keys.astype(jnp.float32))
    s1, i1 = lax.top_k(s[:, :, 0], PEER_TOPK)
    s2, i2 = lax.top_k(s[:, :, 1], PEER_TOPK)
    cand = (s1[..., :, None] + s2[..., None, :]).reshape(n, PEER_HEADS, PEER_TOPK * PEER_TOPK)
    cand_idx = (i1[..., :, None] * PEER_NKEYS + i2[..., None, :]).reshape(n, PEER_HEADS, PEER_TOPK * PEER_TOPK)
    top, pos = lax.top_k(cand, PEER_TOPK)
    experts = jnp.take_along_axis(cand_idx, pos, axis=-1)
    g = jax.nn.softmax(top, axis=-1)
    hid = jax.nn.gelu(jnp.einsum('nd,nhkd->nhk', c, u_tab[experts]).astype(jnp.float32), approximate=False)
    return jnp.einsum('nhk,nhkd->nd', (g * hid).astype(c.dtype), v_tab[experts])


def peer_ffn(c, w_q, sub_keys, u_tab, v_tab):
    shape = c.shape
    c2 = c.reshape(-1, shape[-1])
    n = c2.shape[0]
    blk = min(PEER_TOKEN_BLOCK, n)
    pad = (-n) % blk
    c2 = jnp.pad(c2, ((0, pad), (0, 0))).reshape(-1, blk, shape[-1])
    out = lax.map(lambda cb: peer_tokens(cb, w_q, sub_keys, u_tab, v_tab), c2)
    return out.reshape(-1, shape[-1])[:n].reshape(shape)


def layer(x, p, gla_state, moba_fn, g_mix, w_in, w_gate_up, b_gate, g_gla_out, w_out, g_ffn,
          w_peer_q, peer_sub_keys, peer_u, peer_v, w_ple_gate, w_ple_proj):
    B, T, _ = x.shape
    a = rmsnorm(x, g_mix)
    points = np.cumsum(IN_WIDTHS)[:-1].tolist()
    qg, kg, vg, rg, zg, qm, km, vm = jnp.split(a @ w_in, points, axis=-1)
    log_a = jax.nn.log_sigmoid((zg @ w_gate_up + b_gate).astype(jnp.float32)) / GLA_TAU
    heads_g = lambda t: t.reshape(B, T, GLA_HEADS, -1)
    o_g, s_new = gla_recurrence(heads_g(qg) * GLA_DK ** -0.5, heads_g(kg), heads_g(vg), heads_g(log_a), gla_state)
    o_g = rmsnorm(o_g, g_gla_out).reshape(B, T, GLA_WIDTH) * jax.nn.silu(rg.astype(jnp.float32))
    heads_m = lambda t: t.reshape(B, T, MOBA_HEADS, MOBA_DH)
    k_m, v_m = heads_m(km), heads_m(vm)
    o_m = moba_fn(heads_m(qm), k_m, v_m)
    h = x + jnp.concatenate([o_g.astype(x.dtype), o_m.astype(x.dtype)], axis=-1) @ w_out
    h = h + peer_ffn(rmsnorm(h, g_ffn), w_peer_q, peer_sub_keys, peer_u, peer_v)
    h = h + jax.nn.sigmoid(h @ w_ple_gate) * (p @ w_ple_proj)
    return h, k_m, v_m, s_new


def setup_inputs(seed: int = 0) -> dict:
    key = jax.random.key(seed)
    ks = jax.random.split(key, 24)
    n_pages = PAST_LEN // PAGE_SIZE
    n_used = DEC_BATCH * n_pages
    n_phys = n_used + (n_used + 3) // 4
    D = D_MODEL
    gla_k = GLA_HEADS * GLA_DK

    def nrm(k, shape, s=1.0):
        return s * jax.random.normal(k, shape, jnp.float32)

    return {
        'x_prompt': nrm(ks[0], (BATCH, SEQ, D)),
        'x_sample': nrm(ks[1], (DEC_BATCH, DEC_SEQ, D)),
        'cache_k': nrm(ks[2], (DEPTH, n_phys, PAGE_SIZE, MOBA_HEADS, MOBA_DH)),
        'cache_v': nrm(ks[3], (DEPTH, n_phys, PAGE_SIZE, MOBA_HEADS, MOBA_DH)),
        'state_gla': nrm(ks[4], (DEPTH, DEC_BATCH, GLA_HEADS, GLA_DK, GLA_DV)),
        'page_table': jax.random.permutation(ks[5], n_phys)[:n_used].reshape(DEC_BATCH, n_pages).astype(jnp.int32),
        'p_prompt': nrm(ks[6], (DEPTH, BATCH, SEQ, PLE_DIM)),
        'p_sample': nrm(ks[7], (DEPTH, DEC_BATCH, DEC_SEQ, PLE_DIM)),
        'g_mix': 1.0 + nrm(ks[8], (DEPTH, D), 0.1),
        'w_in': nrm(ks[9], (DEPTH, D, IN_TOTAL), D ** -0.5),
        'w_gate_up': nrm(ks[10], (DEPTH, GLA_GATE_RANK, gla_k), GLA_GATE_RANK ** -0.5),
        'b_gate': nrm(ks[11], (DEPTH, gla_k), 0.1),
        'g_gla_out': 1.0 + nrm(ks[12], (DEPTH, GLA_DV), 0.1),
        'w_out': nrm(ks[13], (DEPTH, MIX_WIDTH, D), MIX_WIDTH ** -0.5),
        'g_ffn': 1.0 + nrm(ks[14], (DEPTH, D), 0.1),
        'w_peer_q': nrm(ks[15], (DEPTH, D, PEER_HEADS * PEER_QDIM), D ** -0.5),
        'peer_sub_keys': nrm(ks[16], (DEPTH, PEER_HEADS, 2, PEER_NKEYS, PEER_QDIM // 2), (PEER_QDIM // 2) ** -0.5),
        'peer_u': nrm(ks[17], (DEPTH, PEER_EXPERTS, D), D ** -0.5),
        'peer_v': nrm(ks[18], (DEPTH, PEER_EXPERTS, D), PEER_HEADS ** -0.5),
        'w_ple_gate': nrm(ks[19], (DEPTH, D, D), D ** -0.5),
        'w_ple_proj': nrm(ks[20], (DEPTH, PLE_DIM, D), PLE_DIM ** -0.5),
        'g_final': 1.0 + nrm(ks[21], (D,), 0.1),
    }


def reference(x_prompt, x_sample, cache_k, cache_v, state_gla, page_table, p_prompt, p_sample,
              g_mix, w_in, w_gate_up, b_gate, g_gla_out, w_out, g_ffn, w_peer_q, peer_sub_keys,
              peer_u, peer_v, w_ple_gate, w_ple_proj, g_final):
    hp, hs = x_prompt, x_sample
    k_p, v_p, s_p, k_s, v_s, s_s = [], [], [], [], [], []
    for i in range(DEPTH):
        w = (g_mix[i], w_in[i], w_gate_up[i], b_gate[i], g_gla_out[i], w_out[i], g_ffn[i],
             w_peer_q[i], peer_sub_keys[i], peer_u[i], peer_v[i], w_ple_gate[i], w_ple_proj[i])
        s0 = jnp.zeros((hp.shape[0], GLA_HEADS, GLA_DK, GLA_DV), jnp.float32)
        hp, kpi, vpi, spi = layer(hp, p_prompt[i], s0, moba_prompt, *w)
        sample_fn = functools.partial(moba_sample, cache_k=cache_k[i], cache_v=cache_v[i], page_table=page_table)
        hs, ksi, vsi, ssi = layer(hs, p_sample[i], state_gla[i], sample_fn, *w)
        k_p.append(kpi); v_p.append(vpi); s_p.append(spi)
        k_s.append(ksi); v_s.append(vsi); s_s.append(ssi)
    y_prompt = rmsnorm(hp, g_final)
    y_sample = rmsnorm(hs, g_final)
    k_prompt, v_prompt, gla_prompt = jnp.stack(k_p), jnp.stack(v_p), jnp.stack(s_p)
    k_sample, v_sample, gla_sample = jnp.stack(k_s), jnp.stack(v_s), jnp.stack(s_s)
    return (y_prompt, y_sample, k_prompt, v_prompt, gla_prompt, k_sample, v_sample, gla_sample)
```

```python
import functools
import math

import jax
import jax.numpy as jnp
from jax import lax
from jax.experimental import pallas as pl
from jax.experimental.pallas import tpu as pltpu

RMS_EPS = 1e-6
GLA_TAU = 16.0
GLA_CHUNK = 16
GLA_GATE_RANK = 16
MOBA_BLOCK = 256
MOBA_TOPK = 3
PEER_TOPK = 16

LANE = 128
SUBLANE = 8
VMEM_LIMIT_BYTES = 56 * 1024 * 1024

F32 = jnp.float32
BF16 = jnp.bfloat16
HIGHEST = lax.Precision.HIGHEST
NT_DIMS = (((1,), (1,)), ((), ()))


def _params(*sem):
    return pltpu.CompilerParams(dimension_semantics=sem, vmem_limit_bytes=VMEM_LIMIT_BYTES)


def _row_tile(n, want):
    if n <= want:
        return n
    return max(t for t in range(LANE, want + 1, LANE) if n % t == 0)


def _norm_kernel(x_ref, g_ref, o_ref):
    x = x_ref[...]
    y = x * lax.rsqrt(jnp.mean(x * x, axis=-1, keepdims=True) + RMS_EPS)
    o_ref[...] = (y * g_ref[...]).astype(o_ref.dtype)


def _rmsnorm_bf16(x, g):
    n, d = x.shape
    tn = _row_tile(n, 512)
    return pl.pallas_call(
        _norm_kernel,
        grid=(n // tn,),
        in_specs=[pl.BlockSpec((tn, d), lambda i: (i, 0)), pl.BlockSpec((1, d), lambda i: (0, 0))],
        out_specs=pl.BlockSpec((tn, d), lambda i: (i, 0)),
        out_shape=jax.ShapeDtypeStruct((n, d), BF16),
        compiler_params=_params("parallel"),
        name="rmsnorm_bf16",
    )(x, g.reshape(1, d))


def _mm_kernel(a_ref, w_ref, o_ref):
    o_ref[...] = jnp.dot(a_ref[...], w_ref[...], preferred_element_type=F32).astype(o_ref.dtype)


def _matmul(a, w, tm, name):
    n, k = a.shape
    m = w.shape[1]
    tn = _row_tile(n, 512)
    assert m % tm == 0
    return pl.pallas_call(
        _mm_kernel,
        grid=(m // tm, n // tn),
        in_specs=[pl.BlockSpec((tn, k), lambda j, i: (i, 0)), pl.BlockSpec((k, tm), lambda j, i: (0, j))],
        out_specs=pl.BlockSpec((tn, tm), lambda j, i: (i, j)),
        out_shape=jax.ShapeDtypeStruct((n, m), F32),
        compiler_params=_params("parallel", "parallel"),
        name=name,
    )(a, w)


def _gla_kernel(q_ref, k_ref, v_ref, r_ref, z_ref, wg_ref, bg_ref, gn_ref, s0_ref, o_ref, st_ref,
                oin_ref, *, heads, dk, dv, t_valid):
    tb = pl.program_id(1)
    tc = q_ref.shape[1]
    nchunk = tc // GLA_CHUNK

    @pl.when(tb == 0)
    def _():
        st_ref[...] = s0_ref[...]

    row = lax.broadcasted_iota(jnp.int32, (tc, 1), 0)
    live = (row + tb * tc) < t_valid

    z = jnp.dot(z_ref[0], wg_ref[...], precision=HIGHEST, preferred_element_type=F32) + bg_ref[...]
    la = jnp.where(live, jax.nn.log_sigmoid(z) / GLA_TAU, 0.0)

    r_i = lax.broadcasted_iota(jnp.int32, (tc, tc), 0)
    c_i = lax.broadcasted_iota(jnp.int32, (tc, tc), 1)
    same = (r_i // GLA_CHUNK) == (c_i // GLA_CHUNK)
    tri = jnp.where(same & (c_i <= r_i), 1.0, 0.0).astype(F32)
    blk = jnp.where(same, 1.0, 0.0).astype(F32)
    b_all = jnp.dot(tri, la, precision=HIGHEST, preferred_element_type=F32)
    bend_all = jnp.dot(blk, la, precision=HIGHEST, preferred_element_type=F32)

    pos = row % GLA_CHUNK
    scale = dk ** -0.5

    for h in range(heads):
        ks = slice(h * dk, (h + 1) * dk)
        vs = slice(h * dv, (h + 1) * dv)
        q = q_ref[0, :, ks] * scale
        k = jnp.where(live, k_ref[0, :, ks], 0.0)
        v = v_ref[0, :, vs]
        b = b_all[:, ks]
        bend = bend_all[:, ks]
        qt = (q * jnp.exp(b)).astype(BF16)
        kt = (k * jnp.exp(bend - b)).astype(BF16)
        dec = jnp.exp(bend)
        vb = v.astype(BF16)

        o_intra = jnp.zeros((tc, dv), F32)
        for j in range(GLA_CHUNK):
            if j == 0:
                kj, bj, vj = k, b, v
            else:
                kj = pltpu.roll(k, j, 0)
                bj = pltpu.roll(b, j, 0)
                vj = pltpu.roll(v, j, 0)
            w = jnp.where(pos >= j, q * kj * jnp.exp(b - bj), 0.0)
            o_intra = o_intra + jnp.sum(w, axis=1, keepdims=True) * vj

        st = st_ref[0, h]
        for c in range(nchunk):
            sl = slice(c * GLA_CHUNK, (c + 1) * GLA_CHUNK)
            oin_ref[sl, :] = lax.dot_general(qt[sl, :], st.astype(BF16), NT_DIMS, preferred_element_type=F32)
            kv = lax.dot_general(vb[sl, :], kt[sl, :], (((0,), (0,)), ((), ())), preferred_element_type=F32)
            st = st * dec[c * GLA_CHUNK:c * GLA_CHUNK + 1, :] + kv
        st_ref[0, h] = st

        o = oin_ref[...] + o_intra
        o = o * lax.rsqrt(jnp.mean(o * o, axis=-1, keepdims=True) + RMS_EPS) * gn_ref[...]
        o_ref[0, :, vs] = (o * jax.nn.silu(r_ref[0, :, vs])).astype(o_ref.dtype)


def _gla(proj, col, wg_pad, b_gate, g_gla, s0t, t_valid):
    bsz, tpad, _ = proj.shape
    _, heads, dv, dk = s0t.shape
    tc = _row_tile(tpad, 256)
    kw, vw = heads * dk, heads * dv
    kern = functools.partial(_gla_kernel, heads=heads, dk=dk, dv=dv, t_valid=t_valid)
    return pl.pallas_call(
        kern,
        grid=(bsz, tpad // tc),
        in_specs=[
            pl.BlockSpec((1, tc, kw), lambda b, t: (b, t, col["qg"] // kw)),
            pl.BlockSpec((1, tc, kw), lambda b, t: (b, t, col["kg"] // kw)),
            pl.BlockSpec((1, tc, vw), lambda b, t: (b, t, col["vg"] // vw)),
            pl.BlockSpec((1, tc, vw), lambda b, t: (b, t, col["rg"] // vw)),
            pl.BlockSpec((1, tc, LANE), lambda b, t: (b, t, col["zg"] // LANE)),
            pl.BlockSpec((LANE, kw), lambda b, t: (0, 0)),
            pl.BlockSpec((1, kw), lambda b, t: (0, 0)),
            pl.BlockSpec((1, dv), lambda b, t: (0, 0)),
            pl.BlockSpec((1, heads, dv, dk), lambda b, t: (b, 0, 0, 0)),
        ],
        out_specs=[
            pl.BlockSpec((1, tc, vw), lambda b, t: (b, t, 0)),
            pl.BlockSpec((1, heads, dv, dk), lambda b, t: (b, 0, 0, 0)),
        ],
        out_shape=[
            jax.ShapeDtypeStruct((bsz, tpad, vw), BF16),
            jax.ShapeDtypeStruct((bsz, heads, dv, dk), F32),
        ],
        scratch_shapes=[pltpu.VMEM((tc, dv), F32)],
        compiler_params=_params("parallel", "arbitrary"),
        name="gla",
    )(proj, proj, proj, proj, proj, wg_pad, b_gate.reshape(1, kw), g_gla.reshape(1, dv), s0t)


def _moba_prompt_kernel(slope_ref, q_ref, k_ref, v_ref, o_ref, means_ref, m_ref, l_ref, acc_ref, *, nblk):
    h = pl.program_id(1)
    qb = pl.program_id(2)
    blk = MOBA_BLOCK
    dh = q_ref.shape[2]
    scale = dh ** -0.5
    slope = slope_ref[h]

    @pl.when(qb == 0)
    def _():
        means_ref[...] = jnp.zeros_like(means_ref)
        for c in range(nblk):
            means_ref[c:c + 1, :] = jnp.mean(k_ref[0, c * blk:(c + 1) * blk, :], axis=0, keepdims=True)

    q = q_ref[0]
    qb16 = q.astype(BF16)

    s = lax.dot_general(q, means_ref[...], NT_DIMS, precision=HIGHEST, preferred_element_type=F32)
    lane = lax.broadcasted_iota(jnp.int32, s.shape, 1)
    elig = lane < qb
    sm = jnp.where(elig, s, -jnp.inf)
    rank = jnp.zeros(s.shape, jnp.int32)
    for c2 in range(nblk - 1):
        col = sm[:, c2:c2 + 1]
        beats = (col > sm) | ((col == sm) & (lane > c2))
        rank = rank + beats.astype(jnp.int32)
    sel = jnp.where(elig & (rank < MOBA_TOPK), 1.0, 0.0)

    qi = lax.broadcasted_iota(jnp.int32, (blk, blk), 0)
    kj = lax.broadcasted_iota(jnp.int32, (blk, blk), 1)
    rel = (qi - kj).astype(F32)

    def scores(kc):
        return lax.dot_general(qb16, kc.astype(BF16), NT_DIMS, preferred_element_type=F32) * scale

    start = pl.multiple_of(qb * blk, blk)
    s_own = scores(k_ref[0, pl.ds(start, blk), :]) - slope * rel
    s_own = jnp.where(qi >= kj, s_own, -jnp.inf)
    m0 = jnp.max(s_own, axis=1, keepdims=True)
    p0 = jnp.exp(s_own - m0)
    m_ref[...] = m0
    l_ref[...] = jnp.sum(p0, axis=1, keepdims=True)
    acc_ref[...] = jnp.dot(p0.astype(BF16), v_ref[0, pl.ds(start, blk), :].astype(BF16), preferred_element_type=F32)

    for c in range(nblk - 1):
        @pl.when(c < qb)
        def _():
            dist = rel + ((qb - c) * blk).astype(F32)
            sc = scores(k_ref[0, c * blk:(c + 1) * blk, :]) - slope * dist
            sc = jnp.where(sel[:, c:c + 1] > 0.5, sc, -jnp.inf)
            m_old = m_ref[...]
            m_new = jnp.maximum(m_old, jnp.max(sc, axis=1, keepdims=True))
            alpha = jnp.exp(m_old - m_new)
            p = jnp.exp(sc - m_new)
            l_ref[...] = alpha * l_ref[...] + jnp.sum(p, axis=1, keepdims=True)
            acc_ref[...] = alpha * acc_ref[...] + jnp.dot(
                p.astype(BF16), v_ref[0, c * blk:(c + 1) * blk, :].astype(BF16), preferred_element_type=F32)
            m_ref[...] = m_new

    o_ref[0] = (acc_ref[...] / l_ref[...]).astype(o_ref.dtype)


def _moba_prompt(proj, q_col, km, vm, slopes):
    bsz, t, _ = proj.shape
    heads = slopes.shape[0]
    dh = km.shape[2] // heads
    assert t % MOBA_BLOCK == 0
    nblk = t // MOBA_BLOCK
    assert nblk <= LANE
    kern = functools.partial(_moba_prompt_kernel, nblk=nblk)
    return pl.pallas_call(
        kern,
        grid=(bsz, heads, nblk),
        in_specs=[
            pl.BlockSpec(memory_space=pltpu.SMEM),
            pl.BlockSpec((1, MOBA_BLOCK, dh), lambda b, h, i: (b, i, q_col // dh + h)),
            pl.BlockSpec((1, t, dh), lambda b, h, i: (b, 0, h)),
            pl.BlockSpec((1, t, dh), lambda b, h, i: (b, 0, h)),
        ],
        out_specs=pl.BlockSpec((1, MOBA_BLOCK, dh), lambda b, h, i: (b, i, h)),
        out_shape=jax.ShapeDtypeStruct((bsz, t, heads * dh), BF16),
        scratch_shapes=[
            pltpu.VMEM((LANE, dh), F32),
            pltpu.VMEM((MOBA_BLOCK, 1), F32),
            pltpu.VMEM((MOBA_BLOCK, 1), F32),
            pltpu.VMEM((MOBA_BLOCK, dh), F32),
        ],
        compiler_params=_params("parallel", "parallel", "arbitrary"),
        name="moba_prompt",
    )(slopes, proj, km, vm)


def _pagesum_kernel(c_ref, o_ref):
    for p in range(c_ref.shape[0]):
        o_ref[0, p:p + 1, :] = jnp.sum(c_ref[p], axis=0, keepdims=True)


def _page_sums(cache):
    n_phys, page, w = cache.shape
    pg = max(d for d in (8, 5, 4, 2, 1) if n_phys % d == 0)
    out = pl.pallas_call(
        _pagesum_kernel,
        grid=(n_phys // pg,),
        in_specs=[pl.BlockSpec((pg, page, w), lambda i: (i, 0, 0))],
        out_specs=pl.BlockSpec((1, pg, w), lambda i: (i, 0, 0)),
        out_shape=jax.ShapeDtypeStruct((n_phys // pg, pg, w), F32),
        compiler_params=_params("parallel"),
        name="page_sums",
    )(cache)
    return out.reshape(n_phys, w)


def _sample_select_kernel(pt_ref, q_ref, ps_ref, o_ref, means_ref, *, nbf, ppb, heads, n_pages):
    b = pl.program_id(0)
    dh = q_ref.shape[2] // heads
    means_ref[...] = jnp.zeros_like(means_ref)

    def fill(c, carry):
        acc = ps_ref[pl.ds(pt_ref[b * n_pages + c * ppb], 1), :]
        for j in range(1, ppb):
            acc = acc + ps_ref[pl.ds(pt_ref[b * n_pages + c * ppb + j], 1), :]
        means_ref[pl.ds(c, 1), :] = acc / MOBA_BLOCK
        return carry

    lax.fori_loop(0, nbf, fill, 0)

    npad = means_ref.shape[0]
    lane = lax.broadcasted_iota(jnp.int32, (SUBLANE, npad), 1)
    qpad = q_ref[0]
    for h in range(heads):
        hs = slice(h * dh, (h + 1) * dh)
        s = lax.dot_general(qpad[:, hs], means_ref[:, hs], NT_DIMS, precision=HIGHEST, preferred_element_type=F32)
        s = jnp.where(lane < nbf, s, -jnp.inf)
        out = jnp.zeros((SUBLANE, LANE), jnp.int32)
        out_lane = lax.broadcasted_iota(jnp.int32, (SUBLANE, LANE), 1)
        for r in range(MOBA_TOPK):
            m = jnp.max(s, axis=1, keepdims=True)
            idx = jnp.min(jnp.where(s == m, lane, npad), axis=1, keepdims=True)
            s = jnp.where(lane == idx, -jnp.inf, s)
            out = jnp.where(out_lane == r, idx, out)
        o_ref[0, h] = out


def _sample_select(q, page_sums, pt_flat, heads, nbf, ppb, n_pages):
    db, tq, w = q.shape
    assert tq <= SUBLANE
    q = jnp.pad(q, ((0, 0), (0, SUBLANE - tq), (0, 0)))
    tq = SUBLANE
    n_phys = page_sums.shape[0]
    npad = -(-nbf // LANE) * LANE
    kern = functools.partial(_sample_select_kernel, nbf=nbf, ppb=ppb, heads=heads, n_pages=n_pages)
    return pl.pallas_call(
        kern,
        grid_spec=pltpu.PrefetchScalarGridSpec(
            num_scalar_prefetch=1,
            grid=(db,),
            in_specs=[
                pl.BlockSpec((1, tq, w), lambda b, pt: (b, 0, 0)),
                pl.BlockSpec((n_phys, w), lambda b, pt: (0, 0)),
            ],
            out_specs=pl.BlockSpec((1, heads, SUBLANE, LANE), lambda b, pt: (b, 0, 0, 0)),
            scratch_shapes=[pltpu.VMEM((npad, w), F32)],
        ),
        out_shape=jax.ShapeDtypeStruct((db, heads, SUBLANE, LANE), jnp.int32),
        compiler_params=_params("arbitrary"),
        name="moba_sample_select",
    )(pt_flat, q, page_sums)


def _sample_attend_kernel(pt_ref, idx_ref, slope_ref, q_ref, kn_ref, vn_ref, *rest, heads, tq, ppb, past_len, page):
    nsel = MOBA_TOPK * ppb
    kc_refs = rest[:nsel]
    vc_refs = rest[nsel:2 * nsel]
    o_ref = rest[2 * nsel]
    b, h, t = pl.program_id(0), pl.program_id(1), pl.program_id(2)
    dh = q_ref.shape[2]
    scale = dh ** -0.5
    slope = slope_ref[h]
    q = q_ref[0, pl.ds(t, 1), :]
    q_pos = past_len + t

    row = lax.broadcasted_iota(jnp.int32, (page, 1), 0)
    s_list = []
    for r in range(MOBA_TOPK):
        blk_idx = idx_ref[((b * heads + h) * tq + t) * MOBA_TOPK + r]
        for j in range(ppb):
            kpos = blk_idx * MOBA_BLOCK + j * page + row
            dist = (q_pos - kpos).astype(F32)
            dots = jnp.sum(kc_refs[r * ppb + j][0] * q, axis=1, keepdims=True)
            s_list.append(dots * scale - slope * dist)
    own_row = lax.broadcasted_iota(jnp.int32, (tq, 1), 0)
    s_own = jnp.sum(kn_ref[0] * q, axis=1, keepdims=True) * scale - slope * (t - own_row).astype(F32)
    s_own = jnp.where(own_row <= t, s_own, -jnp.inf)

    m = jnp.max(s_own, axis=0, keepdims=True)
    for s in s_list:
        m = jnp.maximum(m, jnp.max(s, axis=0, keepdims=True))
    p_own = jnp.exp(s_own - m)
    den = jnp.sum(p_own, axis=0, keepdims=True)
    num = jnp.sum(p_own * vn_ref[0], axis=0, keepdims=True)
    for i, s in enumerate(s_list):
        p = jnp.exp(s - m)
        den = den + jnp.sum(p, axis=0, keepdims=True)
        num = num + jnp.sum(p * vc_refs[i][0], axis=0, keepdims=True)
    o_ref[0] = num / den


def _sample_attend(q, kn, vn, cache_k, cache_v, pt_flat, idx_flat, slopes, n_pages, past_len):
    db, tq, w = q.shape
    heads = slopes.shape[0]
    dh = w // heads
    page = cache_k.shape[1]
    ppb = MOBA_BLOCK // page

    def cache_spec(r, j):
        def imap(b, h, t, pt, idx):
            blk_idx = idx[((b * heads + h) * tq + t) * MOBA_TOPK + r]
            return (pt[b * n_pages + blk_idx * ppb + j], 0, h)
        return pl.BlockSpec((1, page, dh), imap)

    tok = pl.BlockSpec((1, tq, dh), lambda b, h, t, pt, idx: (b, 0, h))
    cache_specs = [cache_spec(r, j) for r in range(MOBA_TOPK) for j in range(ppb)]
    kern = functools.partial(_sample_attend_kernel, heads=heads, tq=tq, ppb=ppb, past_len=past_len, page=page)
    nsel = MOBA_TOPK * ppb
    out = pl.pallas_call(
        kern,
        grid_spec=pltpu.PrefetchScalarGridSpec(
            num_scalar_prefetch=2,
            grid=(db, heads, tq),
            in_specs=[pl.BlockSpec(memory_space=pltpu.SMEM), tok, tok, tok] + cache_specs + cache_specs,
            out_specs=pl.BlockSpec((1, 1, dh), lambda b, h, t, pt, idx: ((b * heads + h) * tq + t, 0, 0)),
        ),
        out_shape=jax.ShapeDtypeStruct((db * heads * tq, 1, dh), F32),
        compiler_params=_params("parallel", "parallel", "parallel"),
        name="moba_sample_attend",
    )(pt_flat, idx_flat, slopes, q, kn, vn, *([cache_k] * nsel), *([cache_v] * nsel))
    return out.reshape(db, heads, tq, dh).transpose(0, 2, 1, 3).reshape(db, tq, w).astype(BF16)


def _outproj_kernel(og_ref, om_ref, w_ref, x_ref, g_ref, h_ref, c_ref):
    gw = og_ref.shape[1]
    acc = jnp.dot(og_ref[...], w_ref[0:gw, :], preferred_element_type=F32)
    acc = acc + jnp.dot(om_ref[...], w_ref[gw:, :], preferred_element_type=F32)
    hid = x_ref[...] + acc
    h_ref[...] = hid
    c = hid * lax.rsqrt(jnp.mean(hid * hid, axis=-1, keepdims=True) + RMS_EPS) * g_ref[...]
    c_ref[...] = c.astype(c_ref.dtype)


def _outproj(og, om, w_out, x, g_ffn):
    n, d = x.shape
    tn = _row_tile(n, 256)
    gw, mw = og.shape[1], om.shape[1]
    return pl.pallas_call(
        _outproj_kernel,
        grid=(n // tn,),
        in_specs=[
            pl.BlockSpec((tn, gw), lambda i: (i, 0)),
            pl.BlockSpec((tn, mw), lambda i: (i, 0)),
            pl.BlockSpec((gw + mw, d), lambda i: (0, 0)),
            pl.BlockSpec((tn, d), lambda i: (i, 0)),
            pl.BlockSpec((1, d), lambda i: (0, 0)),
        ],
        out_specs=[pl.BlockSpec((tn, d), lambda i: (i, 0)), pl.BlockSpec((tn, d), lambda i: (i, 0))],
        out_shape=[jax.ShapeDtypeStruct((n, d), F32), jax.ShapeDtypeStruct((n, d), BF16)],
        compiler_params=_params("parallel"),
        name="outproj",
    )(og, om, w_out, x, g_ffn.reshape(1, d))


def _extract_top(x, count):
    rows = lax.broadcasted_iota(jnp.int32, x.shape, 0)
    vals = []
    for _ in range(count):
        m = jnp.max(x, axis=0, keepdims=True)
        idx = jnp.min(jnp.where(x == m, rows, x.shape[0]), axis=0, keepdims=True)
        x = jnp.where(rows == idx, -jnp.inf, x)
        vals.append(m)
    return vals


def _peer_select_kernel(q_ref, kp_ref, s1_ref, s2_ref, st_ref, *, heads):
    nk = kp_ref.shape[2]
    for h in range(heads):
        qh = q_ref[:, h * nk:(h + 1) * nk]
        s1 = lax.dot_general(kp_ref[h, 0], qh, NT_DIMS, precision=HIGHEST, preferred_element_type=F32)
        s2 = lax.dot_general(kp_ref[h, 1], qh, NT_DIMS, precision=HIGHEST, preferred_element_type=F32)
        s1_ref[h] = s1
        s2_ref[h] = s2
        v1 = _extract_top(s1, PEER_TOPK)
        v2 = jnp.concatenate(_extract_top(s2, PEER_TOPK), axis=0)
        cand = jnp.concatenate([a + v2 for a in v1], axis=0)
        top = _extract_top(cand, PEER_TOPK)
        zsum = jnp.ones_like(top[0])
        for t in top[1:]:
            zsum = zsum + jnp.exp(t - top[0])
        thr = top[-1]
        mz = top[0] + jnp.log(zsum)
        st_ref[h] = jnp.concatenate([thr, mz, jnp.zeros((SUBLANE - 2, thr.shape[1]), F32)], axis=0)


def _peer_select(qp, kpad):
    n, w = qp.shape
    heads, _, nk, _ = kpad.shape
    tn = _row_tile(n, 256)
    kern = functools.partial(_peer_select_kernel, heads=heads)
    sspec = pl.BlockSpec((heads, nk, tn), lambda i: (0, 0, i))
    return pl.pallas_call(
        kern,
        grid=(n // tn,),
        in_specs=[pl.BlockSpec((tn, w), lambda i: (i, 0)), pl.BlockSpec(kpad.shape, lambda i: (0, 0, 0, 0))],
        out_specs=[sspec, sspec, pl.BlockSpec((heads, SUBLANE, tn), lambda i: (0, 0, i))],
        out_shape=[
            jax.ShapeDtypeStruct((heads, nk, n), F32),
            jax.ShapeDtypeStruct((heads, nk, n), F32),
            jax.ShapeDtypeStruct((heads, SUBLANE, n), F32),
        ],
        compiler_params=_params("parallel"),
        name="peer_select",
    )(qp, kpad)


def _peer_dense_kernel(ct_ref, u_ref, vt_ref, s1_ref, s2_ref, st_ref, o_ref, w_ref, *, heads):
    j = pl.program_id(1)
    nk = s2_ref.shape[1]
    groups = s1_ref.shape[1]

    hid = jnp.dot(u_ref[...], ct_ref[...], preferred_element_type=F32)
    for g in range(groups):
        rows = slice(g * nk, (g + 1) * nk)
        gate = None
        for h in range(heads):
            c = s1_ref[h, g:g + 1, :] + s2_ref[h]
            e = jnp.exp(c - st_ref[h, 1:2, :])
            term = jnp.where(c >= st_ref[h, 0:1, :], e, 0.0)
            gate = term if gate is None else gate + term
        x = hid[rows, :]
        act = 0.5 * x * (1.0 + lax.erf(x * (2.0 ** -0.5)))
        w_ref[rows, :] = (gate * act).astype(w_ref.dtype)
    upd = jnp.dot(vt_ref[...], w_ref[...], preferred_element_type=F32)

    @pl.when(j == 0)
    def _():
        o_ref[...] = upd

    @pl.when(j > 0)
    def _():
        o_ref[...] = o_ref[...] + upd


def _peer_dense(ct, u16, vt16, s1t, s2t, stats):
    d, n = ct.shape
    e = u16.shape[0]
    heads, nk, _ = s2t.shape
    tn = _row_tile(n, 512)
    te = SUBLANE * nk
    assert e == nk * nk and e % te == 0
    kern = functools.partial(_peer_dense_kernel, heads=heads)
    return pl.pallas_call(
        kern,
        grid=(n // tn, e // te),
        in_specs=[
            pl.BlockSpec((d, tn), lambda i, j: (0, i)),
            pl.BlockSpec((te, d), lambda i, j: (j, 0)),
            pl.BlockSpec((d, te), lambda i, j: (0, j)),
            pl.BlockSpec((heads, SUBLANE, tn), lambda i, j: (0, j, i)),
            pl.BlockSpec((heads, nk, tn), lambda i, j: (0, 0, i)),
            pl.BlockSpec((heads, SUBLANE, tn), lambda i, j: (0, 0, i)),
        ],
        out_specs=pl.BlockSpec((d, tn), lambda i, j: (0, i)),
        out_shape=jax.ShapeDtypeStruct((d, n), F32),
        scratch_shapes=[pltpu.VMEM((te, tn), BF16)],
        compiler_params=_params("parallel", "arbitrary"),
        name="peer_dense",
    )(ct, u16, vt16, s1t, s2t, stats)


def _final_kernel(h_ref, pt_ref, p_ref, wg_ref, wp_ref, g_ref, y_ref):
    h2 = h_ref[...] + pt_ref[...].T
    gate = jax.nn.sigmoid(jnp.dot(h2.astype(BF16), wg_ref[...], preferred_element_type=F32))
    emb = jnp.dot(p_ref[...].astype(BF16), wp_ref[...], preferred_element_type=F32)
    h3 = h2 + gate * emb
    y_ref[...] = h3 * lax.rsqrt(jnp.mean(h3 * h3, axis=-1, keepdims=True) + RMS_EPS) * g_ref[...]


def _final(h, peer_t, p, wg16, wp16, g_final):
    n, d = h.shape
    pd = p.shape[1]
    tn = _row_tile(n, 256)
    return pl.pallas_call(
        _final_kernel,
        grid=(n // tn,),
        in_specs=[
            pl.BlockSpec((tn, d), lambda i: (i, 0)),
            pl.BlockSpec((d, tn), lambda i: (0, i)),
            pl.BlockSpec((tn, pd), lambda i: (i, 0)),
            pl.BlockSpec((d, d), lambda i: (0, 0)),
            pl.BlockSpec((pd, d), lambda i: (0, 0)),
            pl.BlockSpec((1, d), lambda i: (0, 0)),
        ],
        out_specs=pl.BlockSpec((tn, d), lambda i: (i, 0)),
        out_shape=jax.ShapeDtypeStruct((n, d), F32),
        compiler_params=_params("parallel"),
        name="final",
    )(h, peer_t, p, wg16, wp16, g_final.reshape(1, d))


def _pad_rows(a, mult):
    pad = (-a.shape[0]) % mult
    return a if pad == 0 else jnp.pad(a, ((0, pad),) + ((0, 0),) * (a.ndim - 1))


def _token_stages(og, om, x2, p2, wts):
    h, c = _outproj(og, om, wts["w_out"], x2, wts["g_ffn"])
    qp = _matmul(c, wts["w_peer_q"], wts["w_peer_q"].shape[1], "peer_q")
    s1t, s2t, stats = _peer_select(qp, wts["kpad"])
    peer_t = _peer_dense(c.T, wts["u16"], wts["vt16"], s1t, s2t, stats)
    return _final(h, peer_t, p2, wts["w_ple_gate"], wts["w_ple_proj"], wts["g_final"])


def kernel(x_prompt, x_sample, cache_k, cache_v, state_gla, page_table, p_prompt, p_sample, g_mix, w_in, w_gate_up,
           b_gate, g_gla_out, w_out, g_ffn, w_peer_q, peer_sub_keys, peer_u, peer_v, w_ple_gate, w_ple_proj,
           g_final):
    depth = w_in.shape[0]
    assert depth == 1
    bsz, seq, d = x_prompt.shape
    db, tq, _ = x_sample.shape
    _, _, gla_heads, dk, dv = state_gla.shape
    _, n_phys, page, moba_heads, dh = cache_k.shape
    n_pages = page_table.shape[1]
    past_len = n_pages * page
    gk, gv, mw = gla_heads * dk, gla_heads * dv, moba_heads * dh
    assert past_len % MOBA_BLOCK == 0 and MOBA_BLOCK % page == 0
    ppb = MOBA_BLOCK // page
    nbf = past_len // MOBA_BLOCK

    wi = w_in[0]
    o_qg, o_kg, o_vg, o_rg = 0, gk, 2 * gk, 2 * gk + gv
    o_zg = 2 * gk + 2 * gv
    o_qm = o_zg + GLA_GATE_RANK
    o_km, o_vm = o_qm + mw, o_qm + 2 * mw
    zpad = jnp.pad(wi[:, o_zg:o_qm], ((0, 0), (0, LANE - GLA_GATE_RANK)))
    w_rest = jnp.concatenate([wi[:, :o_zg], zpad, wi[:, o_qm:o_km]], axis=1).astype(BF16)
    col = {"qg": o_qg, "kg": o_kg, "vg": o_vg, "rg": o_rg, "zg": o_zg, "qm": o_zg + LANE}
    w_km = wi[:, o_km:o_vm].astype(BF16)
    w_vm = wi[:, o_vm:o_vm + mw].astype(BF16)
    wg_pad = jnp.pad(w_gate_up[0], ((0, LANE - GLA_GATE_RANK), (0, 0)))
    slopes = 2.0 ** (-8.0 * jnp.arange(1, moba_heads + 1, dtype=F32) / moba_heads)
    sk = peer_sub_keys[0]
    hp, _, nk, half = sk.shape
    assert 2 * half == nk
    kpad = jnp.stack([jnp.pad(sk[:, 0], ((0, 0), (0, 0), (0, half))),
                      jnp.pad(sk[:, 1], ((0, 0), (0, 0), (half, 0)))], axis=1)
    wts = {
        "w_out": w_out[0].astype(BF16), "g_ffn": g_ffn[0], "w_peer_q": w_peer_q[0].astype(BF16), "kpad": kpad,
        "u16": peer_u[0].astype(BF16), "vt16": peer_v[0].T.astype(BF16),
        "w_ple_gate": w_ple_gate[0].astype(BF16), "w_ple_proj": w_ple_proj[0].astype(BF16), "g_final": g_final,
    }
    rest_w = w_rest.shape[1]
    tm_rest = max(t for t in range(LANE, rest_w + 1, LANE) if rest_w % t == 0 and t <= 1536)

    def in_proj(x2):
        a = _rmsnorm_bf16(x2, g_mix[0])
        return (_matmul(a, w_rest, tm_rest, "in_proj_rest"), _matmul(a, w_km, mw, "in_proj_k"),
                _matmul(a, w_vm, mw, "in_proj_v"))

    n_p = bsz * seq
    xp2 = x_prompt.reshape(n_p, d)
    proj, km, vm = in_proj(xp2)
    proj3 = proj.reshape(bsz, seq, rest_w)
    s0 = jnp.zeros((bsz, gla_heads, dv, dk), F32)
    og, st_p = _gla(proj3, col, wg_pad, b_gate[0], g_gla_out[0], s0, seq)
    om = _moba_prompt(proj3, col["qm"], km.reshape(bsz, seq, mw), vm.reshape(bsz, seq, mw), slopes)
    y_p = _token_stages(og.reshape(n_p, gv), om.reshape(n_p, mw), xp2, p_prompt[0].reshape(n_p, -1), wts)

    n_s = db * tq
    xs2 = _pad_rows(x_sample.reshape(n_s, d), LANE)
    proj_s, km_s, vm_s = in_proj(xs2)
    proj_s3 = proj_s[:n_s].reshape(db, tq, rest_w)
    tpad = -(-tq // GLA_CHUNK) * GLA_CHUNK
    proj_g = jnp.pad(proj_s3, ((0, 0), (0, tpad - tq), (0, 0)))
    s0_s = jnp.swapaxes(state_gla[0], -1, -2)
    og_s, st_s = _gla(proj_g, col, wg_pad, b_gate[0], g_gla_out[0], s0_s, tq)
    og_s = og_s[:, :tq].reshape(n_s, gv)

    ck = cache_k[0].reshape(n_phys, page, mw)
    cv = cache_v[0].reshape(n_phys, page, mw)
    qm_s = proj_s3[:, :, col["qm"]:col["qm"] + mw]
    km_s3 = km_s[:n_s].reshape(db, tq, mw)
    vm_s3 = vm_s[:n_s].reshape(db, tq, mw)
    pt_flat = page_table.reshape(-1).astype(jnp.int32)
    sel = _sample_select(qm_s, _page_sums(ck), pt_flat, moba_heads, nbf, ppb, n_pages)
    idx_flat = sel[:, :, :tq, :MOBA_TOPK].reshape(-1)
    om_s = _sample_attend(qm_s, km_s3, vm_s3, ck, cv, pt_flat, idx_flat, slopes, n_pages, past_len)
    y_s = _token_stages(_pad_rows(og_s, LANE), _pad_rows(om_s.reshape(n_s, mw), LANE), xs2,
                        _pad_rows(p_sample[0].reshape(n_s, -1), LANE), wts)

    y_prompt = y_p.reshape(bsz, seq, d)
    y_sample = y_s[:n_s].reshape(db, tq, d)
    k_prompt = km.reshape(1, bsz, seq, moba_heads, dh)
    v_prompt = vm.reshape(1, bsz, seq, moba_heads, dh)
    gla_prompt = jnp.swapaxes(st_p, -1, -2)[None]
    k_sample = km_s3.reshape(1, db, tq, moba_heads, dh)
    v_sample = vm_s3.reshape(1, db, tq, moba_heads, dh)
    gla_sample = jnp.swapaxes(st_s, -1, -2)[None]
    return (y_prompt, y_sample, k_prompt, v_prompt, gla_prompt, k_sample, v_sample, gla_sample)
```

```python
import functools
import math

import jax
import jax.numpy as jnp
from jax import lax
from jax.experimental import pallas as pl
from jax.experimental.pallas import tpu as pltpu

RMS_EPS = 1e-6
GLA_TAU = 16.0
GLA_CHUNK = 16
GLA_GATE_RANK = 16
MOBA_BLOCK = 256
MOBA_TOPK = 3
PEER_TOPK = 16

LANE = 128
SUBLANE = 8
VMEM_LIMIT_BYTES = 56 * 1024 * 1024

LOG2_E = math.log2(math.e)

F32 = jnp.float32
BF16 = jnp.bfloat16
HIGHEST = lax.Precision.HIGHEST
NT_DIMS = (((1,), (1,)), ((), ()))


def _params(*sem):
    return pltpu.CompilerParams(dimension_semantics=sem, vmem_limit_bytes=VMEM_LIMIT_BYTES)


def _row_tile(n, want):
    if n <= want:
        return n
    return max(t for t in range(LANE, want + 1, LANE) if n % t == 0)


def _norm_kernel(x_ref, g_ref, o_ref):
    x = x_ref[...]
    y = x * lax.rsqrt(jnp.mean(x * x, axis=-1, keepdims=True) + RMS_EPS)
    o_ref[...] = (y * g_ref[...]).astype(o_ref.dtype)


def _rmsnorm_bf16(x, g):
    n, d = x.shape
    tn = _row_tile(n, 512)
    return pl.pallas_call(
        _norm_kernel,
        grid=(n // tn,),
        in_specs=[pl.BlockSpec((tn, d), lambda i: (i, 0)), pl.BlockSpec((1, d), lambda i: (0, 0))],
        out_specs=pl.BlockSpec((tn, d), lambda i: (i, 0)),
        out_shape=jax.ShapeDtypeStruct((n, d), BF16),
        compiler_params=_params("parallel"),
        name="rmsnorm_bf16",
    )(x, g.reshape(1, d))


def _mm_kernel(a_ref, w_ref, o_ref):
    o_ref[...] = jnp.dot(a_ref[...], w_ref[...], preferred_element_type=F32).astype(o_ref.dtype)


def _matmul(a, w, tm, name):
    n, k = a.shape
    m = w.shape[1]
    tn = _row_tile(n, 512)
    assert m % tm == 0
    return pl.pallas_call(
        _mm_kernel,
        grid=(m // tm, n // tn),
        in_specs=[pl.BlockSpec((tn, k), lambda j, i: (i, 0)), pl.BlockSpec((k, tm), lambda j, i: (0, j))],
        out_specs=pl.BlockSpec((tn, tm), lambda j, i: (i, j)),
        out_shape=jax.ShapeDtypeStruct((n, m), F32),
        compiler_params=_params("parallel", "parallel"),
        name=name,
    )(a, w)


def _gla_kernel(q_ref, k_ref, v_ref, r_ref, z_ref, wg_ref, bg_ref, gn_ref, s0_ref, o_ref, st_ref,
                oin_ref, *, heads, dk, dv, t_valid):
    tb = pl.program_id(1)
    tc = q_ref.shape[1]
    nchunk = tc // GLA_CHUNK

    @pl.when(tb == 0)
    def _():
        st_ref[...] = s0_ref[...]

    row = lax.broadcasted_iota(jnp.int32, (tc, 1), 0)
    live = (row + tb * tc) < t_valid

    z = jnp.dot(z_ref[0], wg_ref[...], precision=HIGHEST, preferred_element_type=F32) + bg_ref[...]
    la = jnp.where(live, jax.nn.log_sigmoid(z) / GLA_TAU, 0.0)

    r_i = lax.broadcasted_iota(jnp.int32, (tc, tc), 0)
    c_i = lax.broadcasted_iota(jnp.int32, (tc, tc), 1)
    same = (r_i // GLA_CHUNK) == (c_i // GLA_CHUNK)
    tri = jnp.where(same & (c_i <= r_i), 1.0, 0.0).astype(F32)
    blk = jnp.where(same, 1.0, 0.0).astype(F32)
    b_all = jnp.dot(tri, la, precision=HIGHEST, preferred_element_type=F32)
    bend_all = jnp.dot(blk, la, precision=HIGHEST, preferred_element_type=F32)

    pos = row % GLA_CHUNK
    scale = dk ** -0.5

    for h in range(heads):
        ks = slice(h * dk, (h + 1) * dk)
        vs = slice(h * dv, (h + 1) * dv)
        q = q_ref[0, :, ks] * scale
        k = jnp.where(live, k_ref[0, :, ks], 0.0)
        v = v_ref[0, :, vs]
        b = b_all[:, ks]
        bend = bend_all[:, ks]
        qt = (q * jnp.exp(b)).astype(BF16)
        kt = (k * jnp.exp(bend - b)).astype(BF16)
        dec = jnp.exp(bend)
        vb = v.astype(BF16)

        o_intra = jnp.zeros((tc, dv), F32)
        for j in range(GLA_CHUNK):
            if j == 0:
                kj, bj, vj = k, b, v
            else:
                kj = pltpu.roll(k, j, 0)
                bj = pltpu.roll(b, j, 0)
                vj = pltpu.roll(v, j, 0)
            w = jnp.where(pos >= j, q * kj * jnp.exp(b - bj), 0.0)
            o_intra = o_intra + jnp.sum(w, axis=1, keepdims=True) * vj

        st = st_ref[0, h]
        for c in range(nchunk):
            sl = slice(c * GLA_CHUNK, (c + 1) * GLA_CHUNK)
            oin_ref[sl, :] = lax.dot_general(qt[sl, :], st.astype(BF16), NT_DIMS, preferred_element_type=F32)
            kv = lax.dot_general(vb[sl, :], kt[sl, :], (((0,), (0,)), ((), ())), preferred_element_type=F32)
            st = st * dec[c * GLA_CHUNK:c * GLA_CHUNK + 1, :] + kv
        st_ref[0, h] = st

        o = oin_ref[...] + o_intra
        o = o * lax.rsqrt(jnp.mean(o * o, axis=-1, keepdims=True) + RMS_EPS) * gn_ref[...]
        o_ref[0, :, vs] = (o * jax.nn.silu(r_ref[0, :, vs])).astype(o_ref.dtype)


def _gla(proj, col, wg_pad, b_gate, g_gla, s0t, t_valid):
    bsz, tpad, _ = proj.shape
    _, heads, dv, dk = s0t.shape
    tc = _row_tile(tpad, 256)
    kw, vw = heads * dk, heads * dv
    kern = functools.partial(_gla_kernel, heads=heads, dk=dk, dv=dv, t_valid=t_valid)
    return pl.pallas_call(
        kern,
        grid=(bsz, tpad // tc),
        in_specs=[
            pl.BlockSpec((1, tc, kw), lambda b, t: (b, t, col["qg"] // kw)),
            pl.BlockSpec((1, tc, kw), lambda b, t: (b, t, col["kg"] // kw)),
            pl.BlockSpec((1, tc, vw), lambda b, t: (b, t, col["vg"] // vw)),
            pl.BlockSpec((1, tc, vw), lambda b, t: (b, t, col["rg"] // vw)),
            pl.BlockSpec((1, tc, LANE), lambda b, t: (b, t, col["zg"] // LANE)),
            pl.BlockSpec((LANE, kw), lambda b, t: (0, 0)),
            pl.BlockSpec((1, kw), lambda b, t: (0, 0)),
            pl.BlockSpec((1, dv), lambda b, t: (0, 0)),
            pl.BlockSpec((1, heads, dv, dk), lambda b, t: (b, 0, 0, 0)),
        ],
        out_specs=[
            pl.BlockSpec((1, tc, vw), lambda b, t: (b, t, 0)),
            pl.BlockSpec((1, heads, dv, dk), lambda b, t: (b, 0, 0, 0)),
        ],
        out_shape=[
            jax.ShapeDtypeStruct((bsz, tpad, vw), BF16),
            jax.ShapeDtypeStruct((bsz, heads, dv, dk), F32),
        ],
        scratch_shapes=[pltpu.VMEM((tc, dv), F32)],
        compiler_params=_params("parallel", "arbitrary"),
        name="gla",
    )(proj, proj, proj, proj, proj, wg_pad, b_gate.reshape(1, kw), g_gla.reshape(1, dv), s0t)


def _moba_prompt_kernel(slope_ref, q_ref, k_ref, v_ref, o_ref, means_ref, *, nblk):
    h = pl.program_id(1)
    qb = pl.program_id(2)
    blk = MOBA_BLOCK
    dh = q_ref.shape[2]
    scale = dh ** -0.5
    slope = slope_ref[h]

    @pl.when(qb == 0)
    def _():
        means_ref[...] = jnp.zeros_like(means_ref)
        for c in range(nblk):
            means_ref[c:c + 1, :] = jnp.mean(k_ref[0, c * blk:(c + 1) * blk, :], axis=0, keepdims=True)

    q = q_ref[0]
    qb16 = q.astype(BF16)

    s = lax.dot_general(q, means_ref[...], NT_DIMS, precision=HIGHEST, preferred_element_type=F32)
    lane = lax.broadcasted_iota(jnp.int32, s.shape, 1)
    elig = lane < qb
    sm = jnp.where(elig, s, -jnp.inf)
    rank = jnp.zeros(s.shape, jnp.int32)
    for c2 in range(nblk - 1):
        col = sm[:, c2:c2 + 1]
        beats = (col > sm) | ((col == sm) & (lane > c2))
        rank = rank + beats.astype(jnp.int32)
    sel = jnp.where(elig & (rank < MOBA_TOPK), 1.0, 0.0)

    qi = lax.broadcasted_iota(jnp.int32, (blk, blk), 0)
    kj = lax.broadcasted_iota(jnp.int32, (blk, blk), 1)
    rel = (qi - kj).astype(F32)

    for c in range(nblk):
        @pl.when(qb == c)
        def _():
            width = (c + 1) * blk
            sc = lax.dot_general(qb16, k_ref[0, 0:width, :].astype(BF16), NT_DIMS, preferred_element_type=F32)
            parts = []
            for c2 in range(c + 1):
                sb = sc[:, c2 * blk:(c2 + 1) * blk] * scale - slope * (rel + float((c - c2) * blk))
                keep = (qi >= kj) if c2 == c else (sel[:, c2:c2 + 1] > 0.5)
                parts.append(jnp.where(keep, sb, -jnp.inf))
            sc = jnp.concatenate(parts, axis=1)
            p = jnp.exp(sc - jnp.max(sc, axis=1, keepdims=True))
            den = jnp.sum(p, axis=1, keepdims=True)
            num = jnp.dot(p.astype(BF16), v_ref[0, 0:width, :].astype(BF16), preferred_element_type=F32)
            o_ref[0] = (num / den).astype(o_ref.dtype)


def _moba_prompt(proj, q_col, km, vm, slopes):
    bsz, t, _ = proj.shape
    heads = slopes.shape[0]
    dh = km.shape[2] // heads
    assert t % MOBA_BLOCK == 0
    nblk = t // MOBA_BLOCK
    assert nblk <= LANE
    kern = functools.partial(_moba_prompt_kernel, nblk=nblk)
    return pl.pallas_call(
        kern,
        grid=(bsz, heads, nblk),
        in_specs=[
            pl.BlockSpec(memory_space=pltpu.SMEM),
            pl.BlockSpec((1, MOBA_BLOCK, dh), lambda b, h, i: (b, i, q_col // dh + h)),
            pl.BlockSpec((1, t, dh), lambda b, h, i: (b, 0, h)),
            pl.BlockSpec((1, t, dh), lambda b, h, i: (b, 0, h)),
        ],
        out_specs=pl.BlockSpec((1, MOBA_BLOCK, dh), lambda b, h, i: (b, i, h)),
        out_shape=jax.ShapeDtypeStruct((bsz, t, heads * dh), BF16),
        scratch_shapes=[pltpu.VMEM((LANE, dh), F32)],
        compiler_params=_params("parallel", "parallel", "arbitrary"),
        name="moba_prompt",
    )(slopes, proj, km, vm)


def _pagesum_kernel(c_ref, o_ref):
    for p in range(c_ref.shape[0]):
        o_ref[p] = jnp.sum(c_ref[p], axis=0)


def _page_sums(cache):
    _, n_phys, page, heads, dh = cache.shape
    pg = max(d for d in (8, 5, 4, 2, 1) if n_phys % d == 0)
    out = pl.pallas_call(
        _pagesum_kernel,
        grid=(n_phys // pg,),
        in_specs=[pl.BlockSpec((None, pg, page, heads, dh), lambda i: (0, i, 0, 0, 0))],
        out_specs=pl.BlockSpec((pg, heads, dh), lambda i: (i, 0, 0)),
        out_shape=jax.ShapeDtypeStruct((n_phys, heads, dh), F32),
        compiler_params=_params("parallel"),
        name="page_sums",
    )(cache)
    return out.reshape(n_phys, heads * dh)


def _sample_select_kernel(pt_ref, q_ref, ps_ref, o_ref, means_ref, *, nbf, ppb, heads, n_pages):
    b = pl.program_id(0)
    dh = q_ref.shape[2] // heads
    means_ref[...] = jnp.zeros_like(means_ref)

    def fill(c, carry):
        acc = ps_ref[pl.ds(pt_ref[b * n_pages + c * ppb], 1), :]
        for j in range(1, ppb):
            acc = acc + ps_ref[pl.ds(pt_ref[b * n_pages + c * ppb + j], 1), :]
        means_ref[pl.ds(c, 1), :] = acc / MOBA_BLOCK
        return carry

    lax.fori_loop(0, nbf, fill, 0)

    npad = means_ref.shape[0]
    lane = lax.broadcasted_iota(jnp.int32, (SUBLANE, npad), 1)
    qpad = q_ref[0]
    for h in range(heads):
        hs = slice(h * dh, (h + 1) * dh)
        s = lax.dot_general(qpad[:, hs], means_ref[:, hs], NT_DIMS, precision=HIGHEST, preferred_element_type=F32)
        s = jnp.where(lane < nbf, s, -jnp.inf)
        out = jnp.zeros((SUBLANE, LANE), jnp.int32)
        out_lane = lax.broadcasted_iota(jnp.int32, (SUBLANE, LANE), 1)
        for r in range(MOBA_TOPK):
            m = jnp.max(s, axis=1, keepdims=True)
            idx = jnp.min(jnp.where(s == m, lane, npad), axis=1, keepdims=True)
            s = jnp.where(lane == idx, -jnp.inf, s)
            out = jnp.where(out_lane == r, idx, out)
        o_ref[0, h] = out


def _sample_select(q, page_sums, pt_flat, heads, nbf, ppb, n_pages):
    db, tq, w = q.shape
    assert tq <= SUBLANE
    q = jnp.pad(q, ((0, 0), (0, SUBLANE - tq), (0, 0)))
    tq = SUBLANE
    n_phys = page_sums.shape[0]
    npad = -(-nbf // LANE) * LANE
    kern = functools.partial(_sample_select_kernel, nbf=nbf, ppb=ppb, heads=heads, n_pages=n_pages)
    return pl.pallas_call(
        kern,
        grid_spec=pltpu.PrefetchScalarGridSpec(
            num_scalar_prefetch=1,
            grid=(db,),
            in_specs=[
                pl.BlockSpec((1, tq, w), lambda b, pt: (b, 0, 0)),
                pl.BlockSpec((n_phys, w), lambda b, pt: (0, 0)),
            ],
            out_specs=pl.BlockSpec((1, heads, SUBLANE, LANE), lambda b, pt: (b, 0, 0, 0)),
            scratch_shapes=[pltpu.VMEM((npad, w), F32)],
        ),
        out_shape=jax.ShapeDtypeStruct((db, heads, SUBLANE, LANE), jnp.int32),
        compiler_params=_params("arbitrary"),
        name="moba_sample_select",
    )(pt_flat, q, page_sums)


def _sample_attend_kernel(pt_ref, idx_ref, slope_ref, q_ref, kn_ref, vn_ref, ck_ref, cv_ref, o_ref,
                          kbuf, vbuf, sem, *, heads, tq, ppb, n_pages, past_len):
    step = pl.program_id(0)
    nsteps = pl.num_programs(0)
    page, dh = kbuf.shape[2], kbuf.shape[3]
    nsel = MOBA_TOPK * ppb
    scale = dh ** -0.5

    def page_copies(st, slot):
        b = st // heads
        h = st % heads
        copies = []
        for t in range(tq):
            for r in range(MOBA_TOPK):
                blk_idx = idx_ref[(st * tq + t) * MOBA_TOPK + r]
                for j in range(ppb):
                    pg = pt_ref[b * n_pages + blk_idx * ppb + j]
                    dst = t * nsel + r * ppb + j
                    copies.append(pltpu.make_async_copy(ck_ref.at[0, pg, :, h, :], kbuf.at[slot, dst], sem.at[slot]))
                    copies.append(pltpu.make_async_copy(cv_ref.at[0, pg, :, h, :], vbuf.at[slot, dst], sem.at[slot]))
        return copies

    slot = step % 2

    @pl.when(step == 0)
    def _():
        for c in page_copies(step, slot):
            c.start()

    @pl.when(step + 1 < nsteps)
    def _():
        for c in page_copies(step + 1, 1 - slot):
            c.start()

    for c in page_copies(step, slot):
        c.wait()

    slope = slope_ref[step % heads]
    row = lax.broadcasted_iota(jnp.int32, (page, 1), 0)
    own_row = lax.broadcasted_iota(jnp.int32, (tq, 1), 0)
    for t in range(tq):
        q = q_ref[0, t:t + 1, :]
        s_list = []
        for r in range(MOBA_TOPK):
            blk_idx = idx_ref[(step * tq + t) * MOBA_TOPK + r]
            for j in range(ppb):
                kpos = blk_idx * MOBA_BLOCK + j * page + row
                dist = (past_len + t - kpos).astype(F32)
                dots = jnp.sum(kbuf[slot, t * nsel + r * ppb + j] * q, axis=1, keepdims=True)
                s_list.append(dots * scale - slope * dist)
        s_own = jnp.sum(kn_ref[0] * q, axis=1, keepdims=True) * scale - slope * (t - own_row).astype(F32)
        s_own = jnp.where(own_row <= t, s_own, -jnp.inf)

        m = jnp.max(s_own, axis=0, keepdims=True)
        for s in s_list:
            m = jnp.maximum(m, jnp.max(s, axis=0, keepdims=True))
        p_own = jnp.exp(s_own - m)
        den = jnp.sum(p_own, axis=0, keepdims=True)
        num = jnp.sum(p_own * vn_ref[0], axis=0, keepdims=True)
        for i, s in enumerate(s_list):
            p = jnp.exp(s - m)
            den = den + jnp.sum(p, axis=0, keepdims=True)
            num = num + jnp.sum(p * vbuf[slot, t * nsel + i], axis=0, keepdims=True)
        o_ref[0, t:t + 1, :] = num / den


def _sample_attend(q, kn, vn, cache_k, cache_v, pt_flat, idx_flat, slopes, n_pages, past_len):
    db, tq, w = q.shape
    heads = slopes.shape[0]
    dh = w // heads
    page = cache_k.shape[2]
    ppb = MOBA_BLOCK // page
    nbuf = tq * MOBA_TOPK * ppb
    tok = pl.BlockSpec((1, tq, dh), lambda s, pt, idx: (s // heads, 0, s % heads))
    kern = functools.partial(_sample_attend_kernel, heads=heads, tq=tq, ppb=ppb, n_pages=n_pages, past_len=past_len)
    out = pl.pallas_call(
        kern,
        grid_spec=pltpu.PrefetchScalarGridSpec(
            num_scalar_prefetch=2,
            grid=(db * heads,),
            in_specs=[pl.BlockSpec(memory_space=pltpu.SMEM), tok, tok, tok,
                      pl.BlockSpec(memory_space=pl.ANY), pl.BlockSpec(memory_space=pl.ANY)],
            out_specs=pl.BlockSpec((1, tq, dh), lambda s, pt, idx: (s, 0, 0)),
            scratch_shapes=[
                pltpu.VMEM((2, nbuf, page, dh), F32),
                pltpu.VMEM((2, nbuf, page, dh), F32),
                pltpu.SemaphoreType.DMA((2,)),
            ],
        ),
        out_shape=jax.ShapeDtypeStruct((db * heads, tq, dh), F32),
        compiler_params=_params("arbitrary"),
        name="moba_sample_attend",
    )(pt_flat, idx_flat, slopes, q, kn, vn, cache_k, cache_v)
    return out.reshape(db, heads, tq, dh).transpose(0, 2, 1, 3).reshape(db, tq, w).astype(BF16)


def _outproj_kernel(og_ref, om_ref, w_ref, x_ref, g_ref, h_ref, c_ref):
    gw = og_ref.shape[1]
    acc = jnp.dot(og_ref[...], w_ref[0:gw, :], preferred_element_type=F32)
    acc = acc + jnp.dot(om_ref[...], w_ref[gw:, :], preferred_element_type=F32)
    hid = x_ref[...] + acc
    h_ref[...] = hid
    c = hid * lax.rsqrt(jnp.mean(hid * hid, axis=-1, keepdims=True) + RMS_EPS) * g_ref[...]
    c_ref[...] = c.astype(c_ref.dtype)


def _outproj(og, om, w_out, x, g_ffn):
    n, d = x.shape
    tn = _row_tile(n, 256)
    gw, mw = og.shape[1], om.shape[1]
    return pl.pallas_call(
        _outproj_kernel,
        grid=(n // tn,),
        in_specs=[
            pl.BlockSpec((tn, gw), lambda i: (i, 0)),
            pl.BlockSpec((tn, mw), lambda i: (i, 0)),
            pl.BlockSpec((gw + mw, d), lambda i: (0, 0)),
            pl.BlockSpec((tn, d), lambda i: (i, 0)),
            pl.BlockSpec((1, d), lambda i: (0, 0)),
        ],
        out_specs=[pl.BlockSpec((tn, d), lambda i: (i, 0)), pl.BlockSpec((tn, d), lambda i: (i, 0))],
        out_shape=[jax.ShapeDtypeStruct((n, d), F32), jax.ShapeDtypeStruct((n, d), BF16)],
        compiler_params=_params("parallel"),
        name="outproj",
    )(og, om, w_out, x, g_ffn.reshape(1, d))


def _sort_pairs(n):
    pairs = []
    p = 1
    while p < n:
        k = p
        while k >= 1:
            for j in range(k % p, n - k, 2 * k):
                for i in range(min(k, n - j - k)):
                    if (i + j) // (2 * p) == (i + j + k) // (2 * p):
                        pairs.append((i + j, i + j + k))
            k //= 2
        p *= 2
    return pairs


def _bitonic_merge(a):
    a = list(a)
    d = len(a) // 2
    while d >= 1:
        for i in range(len(a)):
            if i & d == 0:
                a[i], a[i + d] = jnp.maximum(a[i], a[i + d]), jnp.minimum(a[i], a[i + d])
        d //= 2
    return a


def _merge_top(a, b):
    n = len(a)
    return _bitonic_merge([jnp.maximum(a[i], b[n - 1 - i]) for i in range(n)])


def _merge_sublanes(parts):
    for shift in (4, 2, 1):
        parts = _merge_top(parts, [pltpu.roll(p, shift, 0) for p in parts])
    return parts


def _top_sorted(x):
    parts = [x[SUBLANE * g:SUBLANE * (g + 1), :] for g in range(PEER_TOPK)]
    for i, j in _sort_pairs(PEER_TOPK):
        parts[i], parts[j] = jnp.maximum(parts[i], parts[j]), jnp.minimum(parts[i], parts[j])
    return _merge_sublanes(parts)


def _top_pair_sums(v1, v2):
    sub = lax.broadcasted_iota(jnp.int32, v1[0].shape, 0)
    halves = []
    for k in range(PEER_TOPK // SUBLANE):
        a = v1[k * SUBLANE]
        for s in range(1, SUBLANE):
            a = jnp.where(sub == s, v1[k * SUBLANE + s], a)
        halves.append([a + b for b in v2])
    return _merge_sublanes(_merge_top(halves[0], halves[1]))


def _peer_select_kernel(q_ref, kp_ref, s1_ref, s2_ref, st_ref, *, heads):
    nk = kp_ref.shape[2]
    assert nk == PEER_TOPK * SUBLANE and PEER_TOPK == 2 * SUBLANE
    for h in range(heads):
        qh = q_ref[:, h * nk:(h + 1) * nk]
        s1 = LOG2_E * lax.dot_general(kp_ref[h, 0], qh, NT_DIMS, precision=HIGHEST, preferred_element_type=F32)
        s2 = LOG2_E * lax.dot_general(kp_ref[h, 1], qh, NT_DIMS, precision=HIGHEST, preferred_element_type=F32)
        v1 = _top_sorted(s1)
        v2 = _top_sorted(s2)
        top = _top_pair_sums(v1, v2)
        zsum = jnp.ones_like(top[0])
        for t in top[1:]:
            zsum = zsum + jnp.exp2(t - top[0])
        mz = top[0] + jnp.log2(zsum)
        thr = _top_pair_sums(v1, [b - mz for b in v2])[-1]
        s1_ref[h] = s1
        s2_ref[h] = s2 - mz[0:1, :]
        st_ref[h] = thr


def _peer_select(qp, kpad):
    n, w = qp.shape
    heads, _, nk, _ = kpad.shape
    tn = _row_tile(n, 256)
    kern = functools.partial(_peer_select_kernel, heads=heads)
    sspec = pl.BlockSpec((heads, nk, tn), lambda i: (0, 0, i))
    return pl.pallas_call(
        kern,
        grid=(n // tn,),
        in_specs=[pl.BlockSpec((tn, w), lambda i: (i, 0)), pl.BlockSpec(kpad.shape, lambda i: (0, 0, 0, 0))],
        out_specs=[sspec, sspec, pl.BlockSpec((heads, SUBLANE, tn), lambda i: (0, 0, i))],
        out_shape=[
            jax.ShapeDtypeStruct((heads, nk, n), F32),
            jax.ShapeDtypeStruct((heads, nk, n), F32),
            jax.ShapeDtypeStruct((heads, SUBLANE, n), F32),
        ],
        compiler_params=_params("parallel"),
        name="peer_select",
    )(qp, kpad)


def _peer_dense_kernel(ct_ref, u_ref, vt_ref, s1_ref, s2_ref, thr_ref, o_ref, w_ref, *, heads, n_etiles):
    g = pl.program_id(0)
    slot = g % 2
    nk = s2_ref.shape[1]
    groups = s1_ref.shape[1]
    prev = jnp.maximum(g - 1, 0)

    @pl.when(g == 0)
    def _():
        w_ref[1] = jnp.zeros(w_ref.shape[1:], w_ref.dtype)

    @pl.when(prev % n_etiles == 0)
    def _():
        o_ref[...] = jnp.zeros(o_ref.shape, o_ref.dtype)

    upd = jnp.dot(vt_ref[...], w_ref[1 - slot], preferred_element_type=F32)
    hid = jnp.dot(u_ref[...], ct_ref[...], preferred_element_type=F32)
    for grp in range(groups):
        rows = slice(grp * nk, (grp + 1) * nk)
        gate = None
        for h in range(heads):
            c = s1_ref[h, grp:grp + 1, :] + s2_ref[h]
            term = jnp.where(c >= thr_ref[h, 0:1, :], jnp.exp2(c), 0.0)
            gate = term if gate is None else gate + term
        x = hid[rows, :]
        act = 0.5 * x * (1.0 + lax.erf(x * (2.0 ** -0.5)))
        w_ref[slot, rows, :] = (gate * act).astype(w_ref.dtype)
    o_ref[...] += upd


def _peer_dense(ct, u16, vt16, s1t, s2t, thr):
    d, n = ct.shape
    e = u16.shape[0]
    heads, nk, _ = s2t.shape
    tn = _row_tile(n, 512)
    te = SUBLANE * nk
    assert e == nk * nk and e % te == 0
    n_et = e // te
    n_items = (n // tn) * n_et

    def cur(g):
        a = jnp.minimum(g, n_items - 1)
        return a // n_et, a % n_et

    def prv(g):
        a = jnp.maximum(g - 1, 0)
        return a // n_et, a % n_et

    kern = functools.partial(_peer_dense_kernel, heads=heads, n_etiles=n_et)
    return pl.pallas_call(
        kern,
        grid=(n_items + 1,),
        in_specs=[
            pl.BlockSpec((d, tn), lambda g: (0, cur(g)[0])),
            pl.BlockSpec((te, d), lambda g: (cur(g)[1], 0)),
            pl.BlockSpec((d, te), lambda g: (0, prv(g)[1])),
            pl.BlockSpec((heads, SUBLANE, tn), lambda g: (0, cur(g)[1], cur(g)[0])),
            pl.BlockSpec((heads, nk, tn), lambda g: (0, 0, cur(g)[0])),
            pl.BlockSpec((heads, SUBLANE, tn), lambda g: (0, 0, cur(g)[0])),
        ],
        out_specs=pl.BlockSpec((d, tn), lambda g: (0, prv(g)[0])),
        out_shape=jax.ShapeDtypeStruct((d, n), F32),
        scratch_shapes=[pltpu.VMEM((2, te, tn), BF16)],
        compiler_params=_params("arbitrary"),
        name="peer_dense",
    )(ct, u16, vt16, s1t, s2t, thr)


def _final_kernel(h_ref, pt_ref, p_ref, wg_ref, wp_ref, g_ref, y_ref):
    h2 = h_ref[...] + pt_ref[...].T
    gate = jax.nn.sigmoid(jnp.dot(h2.astype(BF16), wg_ref[...], preferred_element_type=F32))
    emb = jnp.dot(p_ref[...].astype(BF16), wp_ref[...], preferred_element_type=F32)
    h3 = h2 + gate * emb
    y_ref[...] = h3 * lax.rsqrt(jnp.mean(h3 * h3, axis=-1, keepdims=True) + RMS_EPS) * g_ref[...]


def _final(h, peer_t, p, wg16, wp16, g_final):
    n, d = h.shape
    pd = p.shape[1]
    tn = _row_tile(n, 256)
    return pl.pallas_call(
        _final_kernel,
        grid=(n // tn,),
        in_specs=[
            pl.BlockSpec((tn, d), lambda i: (i, 0)),
            pl.BlockSpec((d, tn), lambda i: (0, i)),
            pl.BlockSpec((tn, pd), lambda i: (i, 0)),
            pl.BlockSpec((d, d), lambda i: (0, 0)),
            pl.BlockSpec((pd, d), lambda i: (0, 0)),
            pl.BlockSpec((1, d), lambda i: (0, 0)),
        ],
        out_specs=pl.BlockSpec((tn, d), lambda i: (i, 0)),
        out_shape=jax.ShapeDtypeStruct((n, d), F32),
        compiler_params=_params("parallel"),
        name="final",
    )(h, peer_t, p, wg16, wp16, g_final.reshape(1, d))


def _pad_rows(a, mult):
    pad = (-a.shape[0]) % mult
    return a if pad == 0 else jnp.pad(a, ((0, pad),) + ((0, 0),) * (a.ndim - 1))


def _token_stages(og, om, x2, p2, wts):
    h, c = _outproj(og, om, wts["w_out"], x2, wts["g_ffn"])
    qp = _matmul(c, wts["w_peer_q"], wts["w_peer_q"].shape[1], "peer_q")
    s1t, s2t, thr = _peer_select(qp, wts["kpad"])
    peer_t = _peer_dense(c.T, wts["u16"], wts["vt16"], s1t, s2t, thr)
    return _final(h, peer_t, p2, wts["w_ple_gate"], wts["w_ple_proj"], wts["g_final"])


def kernel(x_prompt, x_sample, cache_k, cache_v, state_gla, page_table, p_prompt, p_sample, g_mix, w_in, w_gate_up,
           b_gate, g_gla_out, w_out, g_ffn, w_peer_q, peer_sub_keys, peer_u, peer_v, w_ple_gate, w_ple_proj,
           g_final):
    depth = w_in.shape[0]
    assert depth == 1
    bsz, seq, d = x_prompt.shape
    db, tq, _ = x_sample.shape
    _, _, gla_heads, dk, dv = state_gla.shape
    _, n_phys, page, moba_heads, dh = cache_k.shape
    n_pages = page_table.shape[1]
    past_len = n_pages * page
    gk, gv, mw = gla_heads * dk, gla_heads * dv, moba_heads * dh
    assert past_len % MOBA_BLOCK == 0 and MOBA_BLOCK % page == 0
    ppb = MOBA_BLOCK // page
    nbf = past_len // MOBA_BLOCK

    wi = w_in[0]
    o_qg, o_kg, o_vg, o_rg = 0, gk, 2 * gk, 2 * gk + gv
    o_zg = 2 * gk + 2 * gv
    o_qm = o_zg + GLA_GATE_RANK
    o_km, o_vm = o_qm + mw, o_qm + 2 * mw
    zpad = jnp.pad(wi[:, o_zg:o_qm], ((0, 0), (0, LANE - GLA_GATE_RANK)))
    w_rest = jnp.concatenate([wi[:, :o_zg], zpad, wi[:, o_qm:o_km]], axis=1).astype(BF16)
    col = {"qg": o_qg, "kg": o_kg, "vg": o_vg, "rg": o_rg, "zg": o_zg, "qm": o_zg + LANE}
    w_km = wi[:, o_km:o_vm].astype(BF16)
    w_vm = wi[:, o_vm:o_vm + mw].astype(BF16)
    wg_pad = jnp.pad(w_gate_up[0], ((0, LANE - GLA_GATE_RANK), (0, 0)))
    slopes = 2.0 ** (-8.0 * jnp.arange(1, moba_heads + 1, dtype=F32) / moba_heads)
    sk = peer_sub_keys[0]
    hp, _, nk, half = sk.shape
    assert 2 * half == nk
    kpad = jnp.stack([jnp.pad(sk[:, 0], ((0, 0), (0, 0), (0, half))),
                      jnp.pad(sk[:, 1], ((0, 0), (0, 0), (half, 0)))], axis=1)
    wts = {
        "w_out": w_out[0].astype(BF16), "g_ffn": g_ffn[0], "w_peer_q": w_peer_q[0].astype(BF16), "kpad": kpad,
        "u16": peer_u[0].astype(BF16), "vt16": peer_v[0].T.astype(BF16),
        "w_ple_gate": w_ple_gate[0].astype(BF16), "w_ple_proj": w_ple_proj[0].astype(BF16), "g_final": g_final,
    }
    rest_w = w_rest.shape[1]
    tm_rest = max(t for t in range(LANE, rest_w + 1, LANE) if rest_w % t == 0 and t <= 1536)

    def in_proj(x2):
        a = _rmsnorm_bf16(x2, g_mix[0])
        return (_matmul(a, w_rest, tm_rest, "in_proj_rest"), _matmul(a, w_km, mw, "in_proj_k"),
                _matmul(a, w_vm, mw, "in_proj_v"))

    n_p = bsz * seq
    xp2 = x_prompt.reshape(n_p, d)
    proj, km, vm = in_proj(xp2)
    proj3 = proj.reshape(bsz, seq, rest_w)
    s0 = jnp.zeros((bsz, gla_heads, dv, dk), F32)
    og, st_p = _gla(proj3, col, wg_pad, b_gate[0], g_gla_out[0], s0, seq)
    om = _moba_prompt(proj3, col["qm"], km.reshape(bsz, seq, mw), vm.reshape(bsz, seq, mw), slopes)
    y_p = _token_stages(og.reshape(n_p, gv), om.reshape(n_p, mw), xp2, p_prompt[0].reshape(n_p, -1), wts)

    n_s = db * tq
    xs2 = _pad_rows(x_sample.reshape(n_s, d), LANE)
    proj_s, km_s, vm_s = in_proj(xs2)
    proj_s3 = proj_s[:n_s].reshape(db, tq, rest_w)
    tpad = -(-tq // GLA_CHUNK) * GLA_CHUNK
    proj_g = jnp.pad(proj_s3, ((0, 0), (0, tpad - tq), (0, 0)))
    s0_s = jnp.swapaxes(state_gla[0], -1, -2)
    og_s, st_s = _gla(proj_g, col, wg_pad, b_gate[0], g_gla_out[0], s0_s, tq)
    og_s = og_s[:, :tq].reshape(n_s, gv)

    qm_s = proj_s3[:, :, col["qm"]:col["qm"] + mw]
    km_s3 = km_s[:n_s].reshape(db, tq, mw)
    vm_s3 = vm_s[:n_s].reshape(db, tq, mw)
    pt_flat = page_table.reshape(-1).astype(jnp.int32)
    sel = _sample_select(qm_s, _page_sums(cache_k), pt_flat, moba_heads, nbf, ppb, n_pages)
    idx_flat = sel[:, :, :tq, :MOBA_TOPK].reshape(-1)
    om_s = _sample_attend(qm_s, km_s3, vm_s3, cache_k, cache_v, pt_flat, idx_flat, slopes, n_pages, past_len)
    y_s = _token_stages(_pad_rows(og_s, LANE), _pad_rows(om_s.reshape(n_s, mw), LANE), xs2,
                        _pad_rows(p_sample[0].reshape(n_s, -1), LANE), wts)

    y_prompt = y_p.reshape(bsz, seq, d)
    y_sample = y_s[:n_s].reshape(db, tq, d)
    k_prompt = km.reshape(1, bsz, seq, moba_heads, dh)
    v_prompt = vm.reshape(1, bsz, seq, moba_heads, dh)
    gla_prompt = jnp.swapaxes(st_p, -1, -2)[None]
    k_sample = km_s3.reshape(1, db, tq, moba_heads, dh)
    v_sample = vm_s3.reshape(1, db, tq, moba_heads, dh)
    gla_sample = jnp.swapaxes(st_s, -1, -2)[None]
    return (y_prompt, y_sample, k_prompt, v_prompt, gla_prompt, k_sample, v_sample, gla_sample)
```
